```python
import math
import jax
import jax.numpy as jnp
from jax import lax
import numpy as np

D_MODEL = 1024
BATCH = 2
SEQ = 8192
DEPTH = 2
DEC_BATCH = 32
DEC_SEQ = 4
PAST_LEN = 8192
PAGE_SIZE = 128

N_HEADS_A = 8
HEAD_DIM = 64
D_A = N_HEADS_A * HEAD_DIM
DILATED_CONFIGS = ((128, 1), (512, 4), (2048, 16))
WIN_MAX = 2048
ATTN_BLOCK = 128
NUM_BUCKETS = 32
MAX_DISTANCE = 2048
D_B = D_MODEL // 2
CONV_B = 31
D_C = D_MODEL
CONV_C = 3
N_GROUPS = 4
EXPERTS_PER_GROUP = 4
N_EXPERTS = N_GROUPS * EXPERTS_PER_GROUP
TOP_K_IN_GROUP = 2
D_EXPERT = D_MODEL // 4
EPS = 1e-6
NEG = -1e30

kernel_name = 'hybrid_dilated_conformer_shortconv_hmoe_step'


def rms_norm(x, g):
    xf = x.astype(jnp.float32)
    y = xf * lax.rsqrt(jnp.mean(xf * xf, axis=-1, keepdims=True) + EPS)
    return (y * g.astype(jnp.float32)).astype(x.dtype)


def layer_norm(x, g, b):
    xf = x.astype(jnp.float32)
    mu = jnp.mean(xf, axis=-1, keepdims=True)
    var = jnp.mean(jnp.square(xf - mu), axis=-1, keepdims=True)
    y = (xf - mu) * lax.rsqrt(var + EPS)
    return (y * g.astype(jnp.float32) + b.astype(jnp.float32)).astype(x.dtype)


def adaln(c, w_ada, b_ada):
    m = jax.nn.silu(c) @ w_ada + b_ada
    return jnp.split(m, 6, axis=-1)


def modulate(h, shift, scale):
    return h * (1.0 + scale[:, None, :]) + shift[:, None, :]


def t5_bucket(distance):
    max_exact = NUM_BUCKETS // 2
    d_f = jnp.maximum(distance, 1).astype(jnp.float32)
    large = max_exact + (jnp.log(d_f / max_exact) / math.log(MAX_DISTANCE / max_exact)
                         * (NUM_BUCKETS - max_exact)).astype(jnp.int32)
    large = jnp.minimum(large, NUM_BUCKETS - 1)
    return jnp.where(distance < max_exact, distance, large)


def rel_bias(distance, table):
    return table[t5_bucket(distance)].astype(jnp.float32)


def causal_dwconv(u_all, w):
    c = u_all.shape[-1]
    return lax.conv_general_dilated(u_all, w[:, None, :].astype(u_all.dtype), window_strides=(1,),
                                    padding='VALID', dimension_numbers=('NWC', 'WIO', 'NWC'),
                                    feature_group_count=c)


def dilated_window_attn_prompt(q, k, v, table, window, dilation):
    bsz, s, h, dh = q.shape
    L = s // dilation
    wr = window // dilation
    Q = ATTN_BLOCK
    nb = -(-L // Q)
    Lp = nb * Q
    n = bsz * dilation

    def to_res(t):
        return t.reshape(bsz, L, dilation, h, dh).transpose(0, 2, 1, 3, 4).reshape(n, L, h, dh)

    qr = jnp.pad(to_res(q), ((0, 0), (0, Lp - L), (0, 0), (0, 0)))
    kr = jnp.pad(to_res(k), ((0, 0), (Q, Lp - L), (0, 0), (0, 0)))
    vr = jnp.pad(to_res(v), ((0, 0), (Q, Lp - L), (0, 0), (0, 0)))
    qb = qr.reshape(n, nb, Q, h, dh)
    kb = jnp.concatenate([kr[:, :Lp].reshape(n, nb, Q, h, dh), kr[:, Q:].reshape(n, nb, Q, h, dh)], axis=2)
    vb = jnp.concatenate([vr[:, :Lp].reshape(n, nb, Q, h, dh), vr[:, Q:].reshape(n, nb, Q, h, dh)], axis=2)
    i = jnp.arange(Q)[:, None]
    j = jnp.arange(2 * Q)[None, :]
    rel = Q + i - j
    blk = jnp.arange(nb)[:, None, None]
    valid = (rel >= 0) & (rel <= wr) & (blk * Q + j - Q >= 0)
    bias = rel_bias(jnp.maximum(rel, 0) * dilation, table).transpose(2, 0, 1)
    sc = jnp.einsum('nbqhd,nbkhd->nbhqk', qb, kb, preferred_element_type=jnp.float32) * (HEAD_DIM ** -0.5)
    sc = jnp.where(valid[None, :, None], sc + bias, NEG)
    m = jnp.max(sc, axis=-1, keepdims=True)
    p = jnp.exp(sc - m)
    l = jnp.sum(p, axis=-1)
    o = jnp.einsum('nbhqk,nbkhd->nbqhd', p, vb.astype(jnp.float32)) / l.transpose(0, 1, 3, 2)[..., None]
    lse = (m[..., 0] + jnp.log(l)).transpose(0, 1, 3, 2)
    o = o.reshape(n, Lp, h, dh)[:, :L].reshape(bsz, dilation, L, h, dh).transpose(0, 2, 1, 3, 4).reshape(bsz, s, h, dh)
    lse = lse.reshape(n, Lp, h)[:, :L].reshape(bsz, dilation, L, h).transpose(0, 2, 1, 3).reshape(bsz, s, h)
    return o, lse


def dilated_window_attn_sample(q, k_all, v_all, table, window, dilation, n_hist):
    t = q.shape[1]
    nj = window // dilation + 1
    i = jnp.arange(t)[:, None]
    j = jnp.arange(nj)[None, :]
    idx = n_hist + i - dilation * j
    valid = idx >= 0
    idx = jnp.maximum(idx, 0)
    kg = k_all[:, idx]
    vg = v_all[:, idx]
    bias = rel_bias(dilation * jnp.arange(nj), table).T
    sc = jnp.einsum('nthd,ntjhd->nhtj', q, kg, preferred_element_type=jnp.float32) * (HEAD_DIM ** -0.5)
    sc = jnp.where(valid[None, None], sc + bias[None, :, None, :], NEG)
    m = jnp.max(sc, axis=-1, keepdims=True)
    p = jnp.exp(sc - m)
    l = jnp.sum(p, axis=-1)
    o = jnp.einsum('nhtj,ntjhd->nthd', p, vg.astype(jnp.float32)) / l.transpose(0, 2, 1)[..., None]
    lse = (m[..., 0] + jnp.log(l)).transpose(0, 2, 1)
    return o, lse


def combine_by_denominator(outs, lses):
    w = jax.nn.softmax(jnp.stack(lses, axis=0), axis=0)
    return jnp.sum(w[..., None] * jnp.stack(outs, axis=0), axis=0)


def attn_a_prompt(q, k, v, table):
    outs, lses = [], []
    for window, dilation in DILATED_CONFIGS:
        o, lse = dilated_window_attn_prompt(q, k, v, table, window, dilation)
        outs.append(o)
        lses.append(lse)
    keep = min(WIN_MAX, q.shape[1])
    return combine_by_denominator(outs, lses).astype(q.dtype), k[:, -keep:], v[:, -keep:]


def attn_a_sample(q, k, v, k_hist, v_hist, table):
    n_hist = k_hist.shape[1]
    k_all = jnp.concatenate([k_hist.astype(k.dtype), k], axis=1)
    v_all = jnp.concatenate([v_hist.astype(v.dtype), v], axis=1)
    outs, lses = [], []
    for window, dilation in DILATED_CONFIGS:
        o, lse = dilated_window_attn_sample(q, k_all, v_all, table, window, dilation, n_hist)
        outs.append(o)
        lses.append(lse)
    return combine_by_denominator(outs, lses).astype(q.dtype), k_all[:, -n_hist:], v_all[:, -n_hist:]


def split_ab(h, w_in):
    n, t, _ = h.shape
    proj = h @ w_in
    q, k, v, ga, gb = jnp.split(proj, [D_A, 2 * D_A, 3 * D_A, 3 * D_A + D_B], axis=-1)
    def heads(z):
        return z.reshape(n, t, N_HEADS_A, HEAD_DIM)
    return heads(q), heads(k), heads(v), ga, gb


def conformer_conv(ga, gb, hist, dw_w, dw_b, ln_g, ln_b):
    u = ga * jax.nn.sigmoid(gb)
    u_all = jnp.concatenate([hist.astype(u.dtype), u], axis=1)
    y = causal_dwconv(u_all, dw_w) + dw_b
    y = jax.nn.silu(layer_norm(y, ln_g, ln_b))
    return y, u_all[:, -(CONV_B - 1):]


def merge_ab(o_attn, y_conv, w_out):
    n, t = y_conv.shape[:2]
    return jnp.concatenate([o_attn.reshape(n, t, D_A), y_conv], axis=-1) @ w_out


def short_conv_mixer(h, hist, w_in, conv_w, w_out):
    bg, cg, hv = jnp.split(h @ w_in, 3, axis=-1)
    u = cg * hv
    u_all = jnp.concatenate([hist.astype(u.dtype), u], axis=1)
    y = bg * causal_dwconv(u_all, conv_w)
    return y @ w_out, u_all[:, -(CONV_C - 1):]


def hier_moe(x, w_rg, b_rg, w_re, b_re, w_gate, w_up, w_down):
    n, t, d = x.shape
    xf = x.reshape(n * t, d)
    lg = (xf @ w_rg).astype(jnp.float32) + b_rg.astype(jnp.float32)
    pg = jax.nn.softmax(lg, axis=-1)
    g_sel = jnp.argmax(lg, axis=-1)
    g_oh = jax.nn.one_hot(g_sel, N_GROUPS, dtype=jnp.float32)
    p_sel = jnp.sum(pg * g_oh, axis=-1, keepdims=True)
    le = jnp.einsum('td,gde->tge', xf, w_re).astype(jnp.float32) + b_re.astype(jnp.float32)
    le_sel = jnp.einsum('tge,tg->te', le, g_oh)
    top_v, top_i = lax.top_k(le_sel, TOP_K_IN_GROUP)
    w_sel = jax.nn.softmax(top_v, axis=-1) * p_sel
    eid = g_sel[:, None] * EXPERTS_PER_GROUP + top_i
    combine = jnp.sum(jax.nn.one_hot(eid, N_EXPERTS, dtype=jnp.float32) * w_sel[..., None], axis=1)
    hg = jnp.einsum('td,edf->tef', xf, w_gate)
    hu = jnp.einsum('td,edf->tef', xf, w_up)
    hh = jax.nn.silu(hg) * hu * combine[..., None].astype(x.dtype)
    return jnp.einsum('tef,efd->td', hh, w_down).reshape(n, t, d)


def setup_inputs(seed: int = 0) -> dict:
    key = jax.random.key(seed)
    ks = jax.random.split(key, 32)
    f32 = jnp.float32
    wbuf = min(WIN_MAX, PAST_LEN)
    def nrm(k, shape, scale):
        return jax.random.normal(k, shape, f32) * scale
    return {
        'x_prompt': nrm(ks[0], (BATCH, SEQ, D_MODEL), 1.0),
        'x_sample': nrm(ks[1], (DEC_BATCH, DEC_SEQ, D_MODEL), 1.0),
        'cache_attn_k': nrm(ks[2], (DEC_BATCH, wbuf, N_HEADS_A, HEAD_DIM), 1.0),
        'cache_attn_v': nrm(ks[3], (DEC_BATCH, wbuf, N_HEADS_A, HEAD_DIM), 1.0),
        'state_conformer_conv': nrm(ks[4], (DEC_BATCH, CONV_B - 1, D_B), 0.5),
        'state_shortconv': nrm(ks[5], (DEC_BATCH, CONV_C - 1, D_C), 0.5),
        'c_prompt': nrm(ks[6], (BATCH, D_MODEL), 1.0),
        'c_sample': nrm(ks[7], (DEC_BATCH, D_MODEL), 1.0),
        'w_ada': nrm(ks[8], (DEPTH, D_MODEL, 6 * D_MODEL), 0.5 * D_MODEL ** -0.5),
        'b_ada': nrm(ks[9], (DEPTH, 6 * D_MODEL), 0.02),
        'norm_mix': 1.0 + nrm(ks[10], (DEPTH, D_MODEL), 0.01),
        'norm_ffn': 1.0 + nrm(ks[11], (DEPTH, D_MODEL), 0.01),
        'norm_final': 1.0 + nrm(ks[12], (D_MODEL,), 0.01),
        'w_in_ab': nrm(ks[13], (D_MODEL, 3 * D_A + 2 * D_B), D_MODEL ** -0.5),
        'rel_bias_table': nrm(ks[14], (NUM_BUCKETS, N_HEADS_A), 0.5),
        'conformer_dw_w': nrm(ks[15], (CONV_B, D_B), CONV_B ** -0.5),
        'conformer_dw_b': nrm(ks[16], (D_B,), 0.02),
        'conformer_ln_g': 1.0 + nrm(ks[17], (D_B,), 0.01),
        'conformer_ln_b': nrm(ks[18], (D_B,), 0.02),
        'w_out_ab': nrm(ks[19], (D_A + D_B, D_MODEL), (D_A + D_B) ** -0.5),
        'w_in_c': nrm(ks[20], (D_MODEL, 3 * D_C), D_MODEL ** -0.5),
        'shortconv_w': nrm(ks[21], (CONV_C, D_C), CONV_C ** -0.5),
        'w_out_c': nrm(ks[22], (D_C, D_MODEL), D_C ** -0.5),
        'moe_w_router_group': nrm(ks[23], (DEPTH, D_MODEL, N_GROUPS), D_MODEL ** -0.5),
        'moe_b_router_group': nrm(ks[24], (DEPTH, N_GROUPS), 0.01),
        'moe_w_router_expert': nrm(ks[25], (DEPTH, N_GROUPS, D_MODEL, EXPERTS_PER_GROUP), D_MODEL ** -0.5),
        'moe_b_router_expert': nrm(ks[26], (DEPTH, N_GROUPS, EXPERTS_PER_GROUP), 0.01),
        'moe_w_gate': nrm(ks[27], (DEPTH, N_EXPERTS, D_MODEL, D_EXPERT), D_MODEL ** -0.5),
        'moe_w_up': nrm(ks[28], (DEPTH, N_EXPERTS, D_MODEL, D_EXPERT), D_MODEL ** -0.5),
        'moe_w_down': nrm(ks[29], (DEPTH, N_EXPERTS, D_EXPERT, D_MODEL), D_EXPERT ** -0.5),
    }


def reference(x_prompt, x_sample, cache_attn_k, cache_attn_v, state_conformer_conv, state_shortconv,
              c_prompt, c_sample, w_ada, b_ada, norm_mix, norm_ffn, norm_final, w_in_ab, rel_bias_table,
              conformer_dw_w, conformer_dw_b, conformer_ln_g, conformer_ln_b, w_out_ab, w_in_c,
              shortconv_w, w_out_c, moe_w_router_group, moe_b_router_group, moe_w_router_expert,
              moe_b_router_expert, moe_w_gate, moe_w_up, moe_w_down):
    xp, xs = x_prompt, x_sample
    for layer in range(DEPTH):
        ap = adaln(c_prompt, w_ada[layer], b_ada[layer])
        asm = adaln(c_sample, w_ada[layer], b_ada[layer])
        hp = modulate(rms_norm(xp, norm_mix[layer]), ap[0], ap[1])
        hs = modulate(rms_norm(xs, norm_mix[layer]), asm[0], asm[1])
        if layer % 2 == 0:
            qp, kp, vp, gap, gbp = split_ab(hp, w_in_ab)
            op, new_k_prompt, new_v_prompt = attn_a_prompt(qp, kp, vp, rel_bias_table)
            hist_p = jnp.zeros((xp.shape[0], CONV_B - 1, D_B), xp.dtype)
            ybp, new_conformer_prompt = conformer_conv(gap, gbp, hist_p, conformer_dw_w, conformer_dw_b,
                                                       conformer_ln_g, conformer_ln_b)
            mp = merge_ab(op, ybp, w_out_ab)
            qs, ks_, vs, gas, gbs = split_ab(hs, w_in_ab)
            osm, new_k_sample, new_v_sample = attn_a_sample(qs, ks_, vs, cache_attn_k, cache_attn_v,
                                                            rel_bias_table)
            ybs, new_conformer_sample = conformer_conv(gas, gbs, state_conformer_conv, conformer_dw_w,
                                                       conformer_dw_b, conformer_ln_g, conformer_ln_b)
            ms = merge_ab(osm, ybs, w_out_ab)
        else:
            hist_p = jnp.zeros((xp.shape[0], CONV_C - 1, D_C), xp.dtype)
            mp, new_shortconv_prompt = short_conv_mixer(hp, hist_p, w_in_c, shortconv_w, w_out_c)
            ms, new_shortconv_sample = short_conv_mixer(hs, state_shortconv, w_in_c, shortconv_w, w_out_c)
        xp = xp + ap[2][:, None, :] * mp
        xs = xs + asm[2][:, None, :] * ms
        hp = modulate(rms_norm(xp, norm_ffn[layer]), ap[3], ap[4])
        hs = modulate(rms_norm(xs, norm_ffn[layer]), asm[3], asm[4])
        moe_args = (moe_w_router_group[layer], moe_b_router_group[layer], moe_w_router_expert[layer],
                    moe_b_router_expert[layer], moe_w_gate[layer], moe_w_up[layer], moe_w_down[layer])
        xp = xp + ap[5][:, None, :] * hier_moe(hp, *moe_args)
        xs = xs + asm[5][:, None, :] * hier_moe(hs, *moe_args)
    y_prompt = rms_norm(xp, norm_final)
    y_sample = rms_norm(xs, norm_final)
    return (y_prompt, y_sample, new_k_prompt, new_v_prompt, new_conformer_prompt, new_shortconv_prompt,
            new_k_sample, new_v_sample, new_conformer_sample, new_shortconv_sample)
```

```python
import functools
import math

import numpy as np
import jax
import jax.numpy as jnp
from jax import lax
from jax.experimental import pallas as pl
from jax.experimental.pallas import tpu as pltpu

F32 = jnp.float32
BF16 = jnp.bfloat16

EPS = 1e-6
NEG = -1e30
VERY_NEG = -3e38
LANES = 128

N_HEADS = 8
HEAD_DIM = 64
DILATED_CONFIGS = ((128, 1), (512, 4), (2048, 16))
ATTN_BLOCK = 128
NUM_BUCKETS = 32
MAX_DISTANCE = 2048
CONV_B = 31
CONV_C = 3
N_GROUPS = 4
EXPERTS_PER_GROUP = 4
N_EXPERTS = N_GROUPS * EXPERTS_PER_GROUP

VMEM_LIMIT = 56 * 1024 * 1024


def _params(*sem):
    return pltpu.CompilerParams(dimension_semantics=sem, vmem_limit_bytes=VMEM_LIMIT)


def _sigmoid(x):
    return 1.0 / (1.0 + jnp.exp(-x))


def _rms_mod(x, g, shift, scale):
    y = x * lax.rsqrt(jnp.mean(x * x, axis=-1, keepdims=True) + EPS)
    return (y * g) * (1.0 + scale) + shift


def _dot(a, b):
    return jnp.dot(a, b, preferred_element_type=F32)


def _dot_nt(a, b):
    return lax.dot_general(a, b, (((1,), (1,)), ((), ())), preferred_element_type=F32)


def _adaln_kernel(c_ref, w_ref, b_ref, o_ref):
    c = c_ref[...]
    s = c * _sigmoid(c)
    o_ref[...] = _dot(s.astype(BF16), w_ref[...].astype(BF16)) + b_ref[...]


def _adaln(c_all, w_ada, b_ada):
    depth, d, n6 = w_ada.shape
    r = c_all.shape[0]
    tn = 1536
    return pl.pallas_call(
        _adaln_kernel,
        grid=(depth, n6 // tn),
        in_specs=[
            pl.BlockSpec((r, d), lambda l, j: (0, 0)),
            pl.BlockSpec((None, d, tn), lambda l, j: (l, 0, j)),
            pl.BlockSpec((None, 1, tn), lambda l, j: (l, 0, j)),
        ],
        out_specs=pl.BlockSpec((None, r, tn), lambda l, j: (l, 0, j)),
        out_shape=jax.ShapeDtypeStruct((depth, r, n6), F32),
        compiler_params=_params("arbitrary", "arbitrary"),
        name="adaln",
    )(c_all, w_ada, b_ada.reshape(depth, 1, n6))


def _mod_spec(arr, tm):
    if arr.shape[1] == 1:
        return pl.BlockSpec((None, 1, arr.shape[2]), lambda b, i: (b, 0, 0))
    return pl.BlockSpec((None, tm, arr.shape[2]), lambda b, i: (b, i, 0))


def _full_spec(arr):
    nd = arr.ndim
    return pl.BlockSpec(arr.shape, lambda b, i: (0,) * nd)


def _tok_spec(tm, width):
    return pl.BlockSpec((None, tm, width), lambda b, i: (b, i, 0))


def _l0_in_kernel(x_ref, sh_ref, sc_ref, g_ref, w_ref, q_ref, k_ref, v_ref, kf_ref, vf_ref, u_ref, *, d_a, d_b):
    h = _rms_mod(x_ref[...], g_ref[...], sh_ref[...], sc_ref[...]).astype(BF16)

    def proj(c0, n):
        return _dot(h, w_ref[:, c0:c0 + n])

    q = proj(0, d_a)
    q_ref[...] = (q * (HEAD_DIM ** -0.5)).astype(q_ref.dtype)
    k = proj(d_a, d_a)
    k_ref[...] = k.astype(BF16)
    kf_ref[...] = k
    v = proj(2 * d_a, d_a)
    v_ref[...] = v.astype(BF16)
    vf_ref[...] = v
    ga = proj(3 * d_a, d_b)
    gb = proj(3 * d_a + d_b, d_b)
    u_ref[...] = ga * _sigmoid(gb)


def _l0_in(x, shift, scale, gnorm, w_bf, tm, keep, q_dtype):
    nb, t, d = x.shape
    d_a = N_HEADS * HEAD_DIM
    d_b = (w_bf.shape[1] - 3 * d_a) // 2
    nt = t // tm
    first_keep = (t - keep) // tm

    def keep_map(b, i):
        return (b, jnp.maximum(i - first_keep, 0), 0)

    outs = pl.pallas_call(
        functools.partial(_l0_in_kernel, d_a=d_a, d_b=d_b),
        grid=(nb, nt),
        in_specs=[_tok_spec(tm, d), _mod_spec(shift, tm), _mod_spec(scale, tm), _full_spec(gnorm), _full_spec(w_bf)],
        out_specs=[
            _tok_spec(tm, d_a), _tok_spec(tm, d_a), _tok_spec(tm, d_a),
            pl.BlockSpec((None, tm, d_a), keep_map), pl.BlockSpec((None, tm, d_a), keep_map),
            _tok_spec(tm, d_b),
        ],
        out_shape=[
            jax.ShapeDtypeStruct((nb, t, d_a), q_dtype),
            jax.ShapeDtypeStruct((nb, t, d_a), BF16),
            jax.ShapeDtypeStruct((nb, t, d_a), BF16),
            jax.ShapeDtypeStruct((nb, keep, d_a), F32),
            jax.ShapeDtypeStruct((nb, keep, d_a), F32),
            jax.ShapeDtypeStruct((nb, t, d_b), F32),
        ],
        compiler_params=_params("arbitrary", "arbitrary"),
        name="l0_in",
    )(x, shift, scale, gnorm, w_bf)
    return outs


def _t5_bucket(distance):
    max_exact = NUM_BUCKETS // 2
    d_f = jnp.maximum(distance, 1).astype(F32)
    large = max_exact + (jnp.log(d_f / max_exact) / math.log(MAX_DISTANCE / max_exact)
                         * (NUM_BUCKETS - max_exact)).astype(jnp.int32)
    large = jnp.minimum(large, NUM_BUCKETS - 1)
    return jnp.where(distance < max_exact, distance, large)


def _bias_kernel(tab_ref, bk_ref, o_ref):
    h = pl.program_id(1)
    bk = bk_ref[...]
    acc = jnp.full(bk.shape, NEG, F32)
    for b in range(NUM_BUCKETS):
        acc = jnp.where(bk == b, tab_ref[b, h], acc)
    o_ref[...] = acc


def _bias_matrices(table, buckets):
    g, r, c = buckets.shape
    return pl.pallas_call(
        _bias_kernel,
        grid=(g, N_HEADS),
        in_specs=[
            pl.BlockSpec(memory_space=pltpu.SMEM),
            pl.BlockSpec((None, r, c), lambda i, h: (i, 0, 0)),
        ],
        out_specs=pl.BlockSpec((None, None, r, c), lambda i, h: (i, h, 0, 0)),
        out_shape=jax.ShapeDtypeStruct((g, N_HEADS, r, c), F32),
        compiler_params=_params("arbitrary", "arbitrary"),
        name="rel_bias",
    )(table, buckets)


def _prompt_buckets():
    q = ATTN_BLOCK
    i = np.arange(q)[:, None]
    j = np.arange(2 * q)[None, :]
    rel = q + i - j
    out = []
    for window, dilation in DILATED_CONFIGS:
        valid = (rel >= 0) & (rel <= window // dilation)
        bk = _t5_bucket(jnp.asarray(np.maximum(rel, 0) * dilation, jnp.int32))
        out.append(jnp.where(jnp.asarray(valid), bk, -1))
    return jnp.stack(out).astype(jnp.int32)


def _sample_buckets(n_hist, t, n_cols):
    i = np.arange(t)[:, None]
    c = np.arange(n_cols)[None, :]
    dist = n_hist + i - c
    out = []
    for window, dilation in DILATED_CONFIGS:
        valid = (dist >= 0) & (dist % dilation == 0) & (dist // dilation <= window // dilation) & (c < n_hist + t)
        bk = _t5_bucket(jnp.asarray(np.maximum(dist, 0), jnp.int32))
        out.append(jnp.where(jnp.asarray(valid), bk, -1))
    return jnp.stack(out).astype(jnp.int32)


def _attn_kernel(*refs, n_cfg, span):
    q_blk = ATTN_BLOCK
    ins = refs[:5 * n_cfg]
    bias_ref = refs[5 * n_cfg]
    o_ref = refs[5 * n_cfg + 1]
    oacc, lacc = refs[5 * n_cfg + 2:]
    s = pl.program_id(1)
    j = pl.program_id(2)
    steps = span // q_blk
    lane = lax.broadcasted_iota(jnp.int32, (q_blk, LANES), 1)
    lo = lane < HEAD_DIM
    col = lax.broadcasted_iota(jnp.int32, (q_blk, 2 * q_blk), 1)
    n_pairs = N_HEADS // 2

    for g, (_, dil) in enumerate(DILATED_CONFIGS):
        q_ref, kp_ref, kc_ref, vp_ref, vc_ref = ins[5 * g:5 * g + 5]
        per_res = steps // dil
        blk = s * per_res + j % per_res
        start = (j // per_res) + dil * q_blk * (j % per_res)
        pen = jnp.where(blk == 0, NEG, 0.0).astype(F32)
        first_pen = jnp.where(col < q_blk, pen, 0.0)
        for p in range(n_pairs):
            sl = slice(LANES * p, LANES * (p + 1))
            qp = q_ref[:, sl]
            kk = jnp.concatenate([kp_ref[:, sl], kc_ref[:, sl]], axis=0)
            vv = jnp.concatenate([vp_ref[:, sl], vc_ref[:, sl]], axis=0)
            outs, lses = [], []
            for hh in range(2):
                qm = jnp.where(lo if hh == 0 else jnp.logical_not(lo), qp, jnp.zeros_like(qp))
                sc = _dot_nt(qm, kk) + bias_ref[g, 2 * p + hh] + first_pen
                m = jnp.max(sc, axis=-1, keepdims=True)
                e = jnp.exp(sc - m)
                l = jnp.sum(e, axis=-1, keepdims=True)
                pv = _dot(e.astype(BF16), vv)
                outs.append(pv * (1.0 / l))
                lses.append(m + jnp.log(l))
            o_pair = jnp.where(lo, outs[0], outs[1])
            lse_pair = jnp.where(lo, lses[0], lses[1])
            if dil == 1:
                rows = pl.ds(pl.multiple_of(start, q_blk), q_blk)
            else:
                rows = pl.ds(start, q_blk, stride=dil)
            oacc[g, p, rows, :] = o_pair
            lacc[g, p, rows, :] = lse_pair

    @pl.when(j == steps - 1)
    def _merge():
        rc = 64
        for p in range(n_pairs):
            def body(c, carry, p=p):
                r0 = pl.multiple_of(c * rc, rc)
                ls = [lacc[g, p, pl.ds(r0, rc), :] for g in range(n_cfg)]
                os_ = [oacc[g, p, pl.ds(r0, rc), :] for g in range(n_cfg)]
                mx = functools.reduce(jnp.maximum, ls)
                ws = [jnp.exp(l - mx) for l in ls]
                den = functools.reduce(lambda a, b: a + b, ws)
                num = functools.reduce(lambda a, b: a + b, [w * o for w, o in zip(ws, os_)])
                o_ref[pl.ds(r0, rc), LANES * p:LANES * (p + 1)] = (num / den).astype(o_ref.dtype)
                return carry
            lax.fori_loop(0, span // rc, body, 0)


def _attn_prompt(q, k, v, bias):
    nb, s_len, da = q.shape
    q_blk = ATTN_BLOCK
    n_cfg = len(DILATED_CONFIGS)
    span = q_blk * max(d for _, d in DILATED_CONFIGS)
    steps = span // q_blk
    nsb = s_len // span
    args, specs = [], []
    for _, dil in DILATED_CONFIGS:
        per_res = steps // dil
        length = s_len // dil

        def cur_map(b, s, j, per_res=per_res):
            return (b, s * per_res + j % per_res, j // per_res)

        def prev_map(b, s, j, per_res=per_res):
            return (b, jnp.maximum(s * per_res + j % per_res - 1, 0), j // per_res)

        view = lambda a, length=length, dil=dil: a.reshape(nb, length, dil * da)
        blk = (None, q_blk, da)
        args += [view(q), view(k), view(k), view(v), view(v)]
        specs += [pl.BlockSpec(blk, cur_map), pl.BlockSpec(blk, prev_map), pl.BlockSpec(blk, cur_map),
                  pl.BlockSpec(blk, prev_map), pl.BlockSpec(blk, cur_map)]
    args.append(bias)
    specs.append(pl.BlockSpec(bias.shape, lambda b, s, j: (0, 0, 0, 0)))
    n_pairs = N_HEADS // 2
    return pl.pallas_call(
        functools.partial(_attn_kernel, n_cfg=n_cfg, span=span),
        grid=(nb, nsb, steps),
        in_specs=specs,
        out_specs=pl.BlockSpec((None, span, da), lambda b, s, j: (b, s, 0)),
        out_shape=jax.ShapeDtypeStruct((nb, s_len, da), BF16),
        scratch_shapes=[pltpu.VMEM((n_cfg, n_pairs, span, LANES), F32),
                        pltpu.VMEM((n_cfg, n_pairs, span, LANES), F32)],
        compiler_params=_params("arbitrary", "arbitrary", "arbitrary"),
        name="attn_prompt",
    )(*args)


def _attn_s_kernel(q_ref, kh_ref, kn_ref, vh_ref, vn_ref, bias_ref, o_ref, *, n_cfg):
    n_hist = kh_ref.shape[0]
    rows = q_ref.shape[1]
    for p in range(N_HEADS // 2):
        sl = slice(LANES * p, LANES * (p + 1))
        q8 = q_ref[p].astype(BF16)
        kh = kh_ref[:, sl].astype(BF16)
        kn = kn_ref[:, sl].astype(BF16)
        sc = jnp.concatenate([_dot_nt(q8, kh), _dot_nt(q8, kn)], axis=1)
        es, lses, linvs = [], [], []
        for g in range(n_cfg):
            sg = sc + bias_ref[g, p]
            m = jnp.max(sg, axis=-1, keepdims=True)
            e = jnp.exp(sg - m)
            l = jnp.sum(e, axis=-1, keepdims=True)
            es.append(e)
            lses.append(m + jnp.log(l))
            linvs.append(1.0 / l)
        es.append(jnp.zeros_like(es[0]))
        pm = jnp.concatenate(es, axis=0).astype(BF16)
        pv = _dot(pm[:, :n_hist], vh_ref[:, sl].astype(BF16)) + _dot(pm[:, n_hist:], vn_ref[:, sl].astype(BF16))
        mx = functools.reduce(jnp.maximum, lses)
        ws = [jnp.exp(l - mx) for l in lses]
        den = functools.reduce(lambda a, b: a + b, ws)
        num = functools.reduce(lambda a, b: a + b,
                               [ws[g] * (pv[rows * g:rows * (g + 1)] * linvs[g]) for g in range(n_cfg)])
        o_ref[p] = num / den


def _attn_sample(q8, k_hist, k_new, v_hist, v_new, bias):
    n, n_pairs, rows, _ = q8.shape
    n_hist, da = k_hist.shape[1:]
    n_new = k_new.shape[1]
    n_cfg = len(DILATED_CONFIGS)
    return pl.pallas_call(
        functools.partial(_attn_s_kernel, n_cfg=n_cfg),
        grid=(n,),
        in_specs=[
            pl.BlockSpec((None, n_pairs, rows, LANES), lambda i: (i, 0, 0, 0)),
            pl.BlockSpec((None, n_hist, da), lambda i: (i, 0, 0)),
            pl.BlockSpec((None, n_new, da), lambda i: (i, 0, 0)),
            pl.BlockSpec((None, n_hist, da), lambda i: (i, 0, 0)),
            pl.BlockSpec((None, n_new, da), lambda i: (i, 0, 0)),
            pl.BlockSpec(bias.shape, lambda i: (0, 0, 0, 0)),
        ],
        out_specs=pl.BlockSpec((None, n_pairs, rows, LANES), lambda i: (i, 0, 0, 0)),
        out_shape=jax.ShapeDtypeStruct((n, n_pairs, rows, LANES), F32),
        compiler_params=_params("arbitrary"),
        name="attn_sample",
    )(q8, k_hist, k_new, v_hist, v_new, bias)


def _conv_s_kernel(u_ref, w_ref, o_ref):
    t = o_ref.shape[0]
    for i in range(t):
        acc = w_ref[0:1, :] * u_ref[i]
        for k in range(1, CONV_B):
            acc = acc + w_ref[k:k + 1, :] * u_ref[i + k]
        o_ref[i] = acc


def _conv_sample(u_all_t, w):
    rows, n, c = u_all_t.shape
    t = rows - (CONV_B - 1)
    return pl.pallas_call(
        _conv_s_kernel,
        out_shape=jax.ShapeDtypeStruct((t, n, c), F32),
        name="conv_sample",
    )(u_all_t, w)


def _route(logits):
    lane = lax.broadcasted_iota(jnp.int32, logits.shape, 1).astype(F32)
    big = 1000.0
    is_g = (lane >= N_EXPERTS) & (lane < N_EXPERTS + N_GROUPS)
    lg = jnp.where(is_g, logits, VERY_NEG)
    gm = jnp.max(lg, axis=-1, keepdims=True)
    sumexp = jnp.sum(jnp.where(is_g, jnp.exp(lg - gm), 0.0), axis=-1, keepdims=True)
    p_sel = 1.0 / sumexp
    gidx = jnp.min(jnp.where(is_g & (logits == gm), lane, big), axis=-1, keepdims=True) - N_EXPERTS
    e_lo = gidx * EXPERTS_PER_GROUP
    sel = (lane >= e_lo) & (lane < e_lo + EXPERTS_PER_GROUP)
    le = jnp.where(sel, logits, VERY_NEG)
    v1 = jnp.max(le, axis=-1, keepdims=True)
    i1 = jnp.min(jnp.where(sel & (logits == v1), lane, big), axis=-1, keepdims=True)
    sel2 = sel & (lane != i1)
    le2 = jnp.where(sel2, logits, VERY_NEG)
    v2 = jnp.max(le2, axis=-1, keepdims=True)
    i2 = jnp.min(jnp.where(sel2 & (logits == v2), lane, big), axis=-1, keepdims=True)
    e2 = jnp.exp(v2 - v1)
    den = 1.0 + e2
    w1 = (1.0 / den) * p_sel
    w2 = (e2 / den) * p_sel
    comb = jnp.where(lane == i1, w1, jnp.where(lane == i2, w2, 0.0))
    return jnp.where(lane == N_EXPERTS, gidx, comb)


def _router_weights(w_rg, b_rg, w_re, b_re):
    d = w_rg.shape[0]
    w_e = jnp.transpose(w_re, (1, 0, 2)).reshape(d, N_EXPERTS)
    w = jnp.concatenate([w_e, w_rg, jnp.zeros((d, LANES - N_EXPERTS - N_GROUPS), F32)], axis=1)
    b = jnp.concatenate([b_re.reshape(N_EXPERTS), b_rg, jnp.zeros((LANES - N_EXPERTS - N_GROUPS,), F32)])
    return w.astype(BF16), b.reshape(1, LANES)


def _ffn_prep(x1, nf_ref, sh_ref, sc_ref, wr_ref, br_ref, hp_ref, rt_ref):
    hp = _rms_mod(x1, nf_ref[...], sh_ref[...], sc_ref[...]).astype(BF16)
    hp_ref[...] = hp
    logits = _dot(hp, wr_ref[...]) + br_ref[...]
    rt_ref[...] = _route(logits)


def _l0_out_kernel(*refs, conv, tm, d_a):
    if conv:
        (o_ref, uc_ref, up_ref, x_ref, g2_ref, sh_ref, sc_ref, cw_ref, cb_ref, lng_ref, lnb_ref, wo_ref, nf_ref,
         wr_ref, br_ref, x1_ref, hp_ref, rt_ref, ubuf) = refs
        halo = up_ref.shape[0]
        i = pl.program_id(1)
        ubuf[0:halo, :] = jnp.where(i == 0, 0.0, up_ref[...])
        ubuf[halo:, :] = uc_ref[...]
        off = halo - (CONV_B - 1)
        rc = 32
        chunks = []
        for c in range(tm // rc):
            acc = cw_ref[0:1, :] * ubuf[off + rc * c:off + rc * c + rc, :]
            for k in range(1, CONV_B):
                acc = acc + cw_ref[k:k + 1, :] * ubuf[off + k + rc * c:off + k + rc * c + rc, :]
            chunks.append(acc)
        y = jnp.concatenate(chunks, axis=0)
    else:
        (o_ref, y_ref, x_ref, g2_ref, sh_ref, sc_ref, cb_ref, lng_ref, lnb_ref, wo_ref, nf_ref,
         wr_ref, br_ref, x1_ref, hp_ref, rt_ref) = refs
        y = y_ref[...]
    y = y + cb_ref[...]
    mu = jnp.mean(y, axis=-1, keepdims=True)
    var = jnp.mean(jnp.square(y - mu), axis=-1, keepdims=True)
    yn = (y - mu) * lax.rsqrt(var + EPS) * lng_ref[...] + lnb_ref[...]
    yb = (yn * _sigmoid(yn)).astype(BF16)
    m = _dot(o_ref[...], wo_ref[0:d_a, :]) + _dot(yb, wo_ref[d_a:, :])
    x1 = x_ref[...] + g2_ref[...] * m
    x1_ref[...] = x1
    _ffn_prep(x1, nf_ref, sh_ref, sc_ref, wr_ref, br_ref, hp_ref, rt_ref)


def _l0_out(o_attn, u_or_y, x, gate, shift, scale, cw, cb, lng, lnb, wo_bf, nf, wr, br, tm, conv):
    nb, t, d = x.shape
    d_a = o_attn.shape[2]
    d_b = u_or_y.shape[2]
    nt = t // tm
    halo = 32
    args = [o_attn, u_or_y]
    specs = [_tok_spec(tm, d_a), _tok_spec(tm, d_b)]
    if conv:
        args.append(u_or_y)
        specs.append(pl.BlockSpec((None, halo, d_b), lambda b, i: (b, jnp.maximum(i * (tm // halo) - 1, 0), 0)))
    args += [x, gate, shift, scale]
    specs += [_tok_spec(tm, d), _mod_spec(gate, tm), _mod_spec(shift, tm), _mod_spec(scale, tm)]
    small = ([cw] if conv else []) + [cb, lng, lnb, wo_bf, nf, wr, br]
    args += small
    specs += [_full_spec(a) for a in small]
    return pl.pallas_call(
        functools.partial(_l0_out_kernel, conv=conv, tm=tm, d_a=d_a),
        grid=(nb, nt),
        in_specs=specs,
        out_specs=[_tok_spec(tm, d), _tok_spec(tm, d), _tok_spec(tm, LANES)],
        out_shape=[jax.ShapeDtypeStruct((nb, t, d), F32), jax.ShapeDtypeStruct((nb, t, d), BF16),
                   jax.ShapeDtypeStruct((nb, t, LANES), F32)],
        scratch_shapes=[pltpu.VMEM((tm + halo, d_b), F32)] if conv else [],
        compiler_params=_params("arbitrary", "arbitrary"),
        name="l0_out",
    )(*args)


def _moe_kernel(hp_ref, rt_ref, wg_ref, wu_ref, wd_ref, o_ref):
    c = pl.program_id(2)
    hp = hp_ref[...]
    rt = rt_ref[...]
    lane = lax.broadcasted_iota(jnp.int32, rt.shape, 1)
    part = None
    for e in range(EXPERTS_PER_GROUP):
        hg = _dot(hp, wg_ref[e])
        hu = _dot(hp, wu_ref[e])
        ce = jnp.sum(jnp.where(lane == EXPERTS_PER_GROUP * c + e, rt, 0.0), axis=-1, keepdims=True)
        hh = ((hg * _sigmoid(hg)) * hu * ce).astype(BF16)
        pe = _dot(hh, wd_ref[e])
        part = pe if part is None else part + pe

    @pl.when(c == 0)
    def _():
        o_ref[...] = part

    @pl.when(c > 0)
    def _():
        o_ref[...] += part


def _moe(hp, route, wg_bf, wu_bf, wd_bf, layer, tm):
    nb, t, d = hp.shape
    f = wg_bf.shape[3]
    epg = EXPERTS_PER_GROUP
    return pl.pallas_call(
        _moe_kernel,
        grid=(nb, t // tm, N_GROUPS),
        in_specs=[
            pl.BlockSpec((None, tm, d), lambda b, i, c: (b, i, 0)),
            pl.BlockSpec((None, tm, LANES), lambda b, i, c: (b, i, 0)),
            pl.BlockSpec((None, epg, d, f), lambda b, i, c: (layer, c, 0, 0)),
            pl.BlockSpec((None, epg, d, f), lambda b, i, c: (layer, c, 0, 0)),
            pl.BlockSpec((None, epg, f, d), lambda b, i, c: (layer, c, 0, 0)),
        ],
        out_specs=pl.BlockSpec((None, tm, d), lambda b, i, c: (b, i, 0)),
        out_shape=jax.ShapeDtypeStruct((nb, t, d), F32),
        compiler_params=_params("arbitrary", "arbitrary", "arbitrary"),
        name="moe",
    )(hp, route, wg_bf, wu_bf, wd_bf)


def _l1_kernel(*refs, sample, tm, d_c):
    if sample:
        (x1_ref, mo_ref, g5_ref, sh1_ref, sc1_ref, g2_ref, sh2_ref, sc2_ref, nm_ref, wi_ref, cw_ref, wo_ref, nf_ref,
         wr_ref, br_ref, m1_ref, m2_ref, p1_ref, p2_ref, x3_ref, hp_ref, rt_ref, u_ref, ubuf) = refs
    else:
        (x1_ref, mo_ref, g5_ref, sh1_ref, sc1_ref, g2_ref, sh2_ref, sc2_ref, nm_ref, wi_ref, cw_ref, wo_ref, nf_ref,
         wr_ref, br_ref, x3_ref, hp_ref, rt_ref, u_ref, ubuf) = refs
    i = pl.program_id(1)
    pad = 8
    x2 = x1_ref[...] + g5_ref[...] * mo_ref[...]
    h = _rms_mod(x2, nm_ref[...], sh1_ref[...], sc1_ref[...]).astype(BF16)
    bg = _dot(h, wi_ref[:, 0:d_c])
    u = _dot(h, wi_ref[:, d_c:2 * d_c]) * _dot(h, wi_ref[:, 2 * d_c:3 * d_c])

    @pl.when(i == 0)
    def _():
        ubuf[0:pad, :] = jnp.zeros((pad, d_c), F32)

    @pl.when(i > 0)
    def _():
        ubuf[0:pad, :] = ubuf[tm:tm + pad, :]

    ubuf[pad:, :] = u
    um1 = ubuf[pad - 1:pad - 1 + tm, :]
    um2 = ubuf[pad - 2:pad - 2 + tm, :]
    if sample:
        um1 = um1 * m1_ref[...] + p1_ref[...]
        um2 = um2 * m2_ref[...] + p2_ref[...]
    y = bg * (cw_ref[0:1, :] * um2 + cw_ref[1:2, :] * um1 + cw_ref[2:3, :] * u)
    u_ref[...] = u if sample else ubuf[tm:tm + pad, :]
    m = _dot(y.astype(BF16), wo_ref[...])
    x3 = x2 + g2_ref[...] * m
    x3_ref[...] = x3
    _ffn_prep(x3, nf_ref, sh2_ref, sc2_ref, wr_ref, br_ref, hp_ref, rt_ref)


def _l1(x1, mo, mods, nm, wi_bf, cw, wo_bf, nf, wr, br, tm, extra=None):
    nb, t, d = x1.shape
    d_c = wo_bf.shape[0]
    nt = t // tm
    sample = extra is not None
    args = [x1, mo] + list(mods)
    specs = [_tok_spec(tm, d), _tok_spec(tm, d)] + [_mod_spec(a, tm) for a in mods]
    small = [nm, wi_bf, cw, wo_bf, nf, wr, br]
    args += small
    specs += [_full_spec(a) for a in small]
    if sample:
        args += list(extra)
        specs += [_tok_spec(tm, a.shape[2]) for a in extra]
    return pl.pallas_call(
        functools.partial(_l1_kernel, sample=sample, tm=tm, d_c=d_c),
        grid=(nb, nt),
        in_specs=specs,
        out_specs=[_tok_spec(tm, d), _tok_spec(tm, d), _tok_spec(tm, LANES),
                   _tok_spec(tm, d_c) if sample else pl.BlockSpec((None, 8, d_c), lambda b, i: (b, 0, 0))],
        out_shape=[jax.ShapeDtypeStruct((nb, t, d), F32), jax.ShapeDtypeStruct((nb, t, d), BF16),
                   jax.ShapeDtypeStruct((nb, t, LANES), F32),
                   jax.ShapeDtypeStruct((nb, t if sample else 8, d_c), F32)],
        scratch_shapes=[pltpu.VMEM((tm + 8, d_c), F32)],
        compiler_params=_params("arbitrary", "arbitrary"),
        name="l1_mixer",
    )(*args)


def _final_kernel(x_ref, mo_ref, g5_ref, nf_ref, o_ref):
    x = x_ref[...] + g5_ref[...] * mo_ref[...]
    y = x * lax.rsqrt(jnp.mean(x * x, axis=-1, keepdims=True) + EPS)
    o_ref[...] = y * nf_ref[...]


def _final(x3, mo, g5, nf, tm):
    nb, t, d = x3.shape
    return pl.pallas_call(
        _final_kernel,
        grid=(nb, t // tm),
        in_specs=[_tok_spec(tm, d), _tok_spec(tm, d), _mod_spec(g5, tm), _full_spec(nf)],
        out_specs=_tok_spec(tm, d),
        out_shape=jax.ShapeDtypeStruct((nb, t, d), F32),
        compiler_params=_params("arbitrary", "arbitrary"),
        name="final_norm",
    )(x3, mo, g5, nf)


def kernel(x_prompt, x_sample, cache_attn_k, cache_attn_v, state_conformer_conv, state_shortconv, c_prompt, c_sample, w_ada, b_ada, norm_mix, norm_ffn, norm_final, w_in_ab, rel_bias_table, conformer_dw_w, conformer_dw_b, conformer_ln_g, conformer_ln_b, w_out_ab, w_in_c, shortconv_w, w_out_c, moe_w_router_group, moe_b_router_group, moe_w_router_expert, moe_b_router_expert, moe_w_gate, moe_w_up, moe_w_down):
    nb, s_len, d = x_prompt.shape
    ns, ts, _ = x_sample.shape
    n_tok_s = ns * ts
    d_a = N_HEADS * HEAD_DIM
    n_hist = cache_attn_k.shape[1]
    keep = min(max(w for w, _ in DILATED_CONFIGS), s_len)

    n_c = nb + ns
    r_pad = -(-n_c // 8) * 8
    c_all = jnp.concatenate([c_prompt, c_sample, jnp.zeros((r_pad - n_c, d), F32)], axis=0)
    ada = _adaln(c_all, w_ada, b_ada)

    def mods(layer):
        m = ada[layer].reshape(r_pad, 6, d)
        mp = [m[:nb, k].reshape(nb, 1, d) for k in range(6)]
        ms = [jnp.repeat(m[nb:n_c, k], ts, axis=0).reshape(1, n_tok_s, d) for k in range(6)]
        return mp, ms

    row = lambda a: a.reshape(1, -1)
    w_in_ab_bf = w_in_ab.astype(BF16)
    w_out_ab_bf = w_out_ab.astype(BF16)
    w_in_c_bf = w_in_c.astype(BF16)
    w_out_c_bf = w_out_c.astype(BF16)
    wg_bf = moe_w_gate.astype(BF16)
    wu_bf = moe_w_up.astype(BF16)
    wd_bf = moe_w_down.astype(BF16)
    routers = [_router_weights(moe_w_router_group[l], moe_b_router_group[l], moe_w_router_expert[l],
                               moe_b_router_expert[l]) for l in range(2)]
    xs = x_sample.reshape(1, n_tok_s, d)

    tm_p = 512
    tm_moe = 1024
    mp0, ms0 = mods(0)
    mp1, ms1 = mods(1)

    q_p, k_p, v_p, new_k_p, new_v_p, u_p = _l0_in(x_prompt, mp0[0], mp0[1], row(norm_mix[0]), w_in_ab_bf, tm_p, keep, BF16)
    bias_p = _bias_matrices(rel_bias_table, _prompt_buckets())
    o_p = _attn_prompt(q_p, k_p, v_p, bias_p)
    x1_p, hp_p, rt_p = _l0_out(o_p, u_p, x_prompt, mp0[2], mp0[3], mp0[4], conformer_dw_w, row(conformer_dw_b),
                               row(conformer_ln_g), row(conformer_ln_b), w_out_ab_bf, row(norm_ffn[0]),
                               routers[0][0], routers[0][1], 256, True)
    mo_p = _moe(hp_p, rt_p, wg_bf, wu_bf, wd_bf, 0, tm_moe)

    q_s, _, _, k_new, v_new, u_s = _l0_in(xs, ms0[0], ms0[1], row(norm_mix[0]), w_in_ab_bf, n_tok_s, n_tok_s, F32)
    n_new_pad = LANES
    pad_new = lambda a: jnp.concatenate([a.reshape(ns, ts, d_a), jnp.zeros((ns, n_new_pad - ts, d_a), F32)], axis=1)
    k_hist = cache_attn_k.reshape(ns, n_hist, d_a)
    v_hist = cache_attn_v.reshape(ns, n_hist, d_a)
    bias_s = _bias_matrices(rel_bias_table, _sample_buckets(n_hist, ts, n_hist + n_new_pad))
    bias_s = bias_s.reshape(len(DILATED_CONFIGS), N_HEADS // 2, 2 * ts, n_hist + n_new_pad)
    n_pairs = N_HEADS // 2
    q5 = q_s.reshape(ns, ts, n_pairs, 2, HEAD_DIM)
    q8 = jnp.einsum('nipcd,hc->nphicd', q5, jnp.eye(2, dtype=F32)).reshape(ns, n_pairs, 2 * ts, LANES)
    o8 = _attn_sample(q8, k_hist, pad_new(k_new), v_hist, pad_new(v_new), bias_s)
    o8 = o8.reshape(ns, n_pairs, 2, ts, 2, HEAD_DIM)
    o_s = jnp.stack([o8[:, :, 0, :, 0], o8[:, :, 1, :, 1]], axis=3)
    o_s = jnp.transpose(o_s, (0, 2, 1, 3, 4))
    u_all = jnp.concatenate([state_conformer_conv, u_s.reshape(ns, ts, -1)], axis=1)
    y_s = _conv_sample(jnp.transpose(u_all, (1, 0, 2)), conformer_dw_w)
    y_s = jnp.transpose(y_s, (1, 0, 2)).reshape(1, n_tok_s, -1)
    x1_s, hp_s, rt_s = _l0_out(o_s.reshape(1, n_tok_s, d_a).astype(BF16), y_s, xs, ms0[2], ms0[3], ms0[4], None,
                               row(conformer_dw_b), row(conformer_ln_g), row(conformer_ln_b), w_out_ab_bf,
                               row(norm_ffn[0]), routers[0][0], routers[0][1], n_tok_s, False)
    mo_s = _moe(hp_s, rt_s, wg_bf, wu_bf, wd_bf, 0, n_tok_s)

    l1_mods_p = (mp0[5], mp1[0], mp1[1], mp1[2], mp1[3], mp1[4])
    x3_p, hp2_p, rt2_p, tail_p = _l1(x1_p, mo_p, l1_mods_p, row(norm_mix[1]), w_in_c_bf, shortconv_w, w_out_c_bf,
                                     row(norm_ffn[1]), routers[1][0], routers[1][1], tm_p)
    mo2_p = _moe(hp2_p, rt2_p, wg_bf, wu_bf, wd_bf, 1, tm_moe)

    pos = np.arange(n_tok_s) % ts
    m1 = jnp.asarray((pos >= 1).astype(np.float32)).reshape(1, n_tok_s, 1)
    m2 = jnp.asarray((pos >= 2).astype(np.float32)).reshape(1, n_tok_s, 1)
    st = state_shortconv
    d_c = st.shape[2]
    zero = jnp.zeros((ns, d_c), F32)
    p1 = jnp.stack([st[:, 1]] + [zero] * (ts - 1), axis=1).reshape(1, n_tok_s, d_c)
    p2 = jnp.stack([st[:, 0], st[:, 1]] + [zero] * (ts - 2), axis=1).reshape(1, n_tok_s, d_c)
    m1 = jnp.broadcast_to(m1, (1, n_tok_s, d_c))
    m2 = jnp.broadcast_to(m2, (1, n_tok_s, d_c))
    l1_mods_s = (ms0[5], ms1[0], ms1[1], ms1[2], ms1[3], ms1[4])
    x3_s, hp2_s, rt2_s, u2_s = _l1(x1_s, mo_s, l1_mods_s, row(norm_mix[1]), w_in_c_bf, shortconv_w, w_out_c_bf,
                                row(norm_ffn[1]), routers[1][0], routers[1][1], n_tok_s, extra=(m1, m2, p1, p2))
    mo2_s = _moe(hp2_s, rt2_s, wg_bf, wu_bf, wd_bf, 1, n_tok_s)

    y_prompt = _final(x3_p, mo2_p, mp1[5], row(norm_final), tm_p)
    y_sample = _final(x3_s, mo2_s, ms1[5], row(norm_final), n_tok_s).reshape(ns, ts, d)

    new_k_prompt = new_k_p.reshape(nb, keep, N_HEADS, HEAD_DIM)
    new_v_prompt = new_v_p.reshape(nb, keep, N_HEADS, HEAD_DIM)
    new_conformer_prompt = u_p[:, -(CONV_B - 1):]
    new_shortconv_prompt = tail_p[:, -(CONV_C - 1):]
    k_new4 = k_new.reshape(ns, ts, N_HEADS, HEAD_DIM)
    v_new4 = v_new.reshape(ns, ts, N_HEADS, HEAD_DIM)
    new_k_sample = jnp.concatenate([cache_attn_k, k_new4], axis=1)[:, -n_hist:]
    new_v_sample = jnp.concatenate([cache_attn_v, v_new4], axis=1)[:, -n_hist:]
    new_conformer_sample = u_all[:, -(CONV_B - 1):]
    new_shortconv_sample = jnp.concatenate([state_shortconv, u2_s.reshape(ns, ts, d_c)], axis=1)[:, -(CONV_C - 1):]
    return (y_prompt, y_sample, new_k_prompt, new_v_prompt, new_conformer_prompt, new_shortconv_prompt,
            new_k_sample, new_v_sample, new_conformer_sample, new_shortconv_sample)
```
